```python
import jax, jax.numpy as jnp
from jax import lax
import numpy as np

D_MODEL = 4096
BATCH = 2
SEQ = 8192
DEPTH = 1

F32 = jnp.float32
EPS = 1e-6

ATT_HEAD_DIM = 64
ATT_HEADS = (D_MODEL // 2) // ATT_HEAD_DIM
ATT_KV_HEADS = ATT_HEADS // 4
ATT_GROUP = ATT_HEADS // ATT_KV_HEADS
WINDOW = 128
ATT_Q_W = ATT_HEADS * ATT_HEAD_DIM
ATT_KV_W = ATT_KV_HEADS * ATT_HEAD_DIM

HGRN_KEY_DIM = 128
HGRN_VAL_DIM = 128
HGRN_HEADS = (D_MODEL // 2) // HGRN_VAL_DIM
HGRN_K_W = HGRN_HEADS * HGRN_KEY_DIM
HGRN_V_W = HGRN_HEADS * HGRN_VAL_DIM
HGRN_CHUNK = 64

IN_WIDTHS = (ATT_Q_W, ATT_KV_W, ATT_KV_W, HGRN_K_W, HGRN_K_W, HGRN_V_W, HGRN_V_W, D_MODEL, D_MODEL)
IN_WIDTH = sum(IN_WIDTHS)
IN_SPLITS = [int(s) for s in np.cumsum(IN_WIDTHS)[:-1]]

N_EXPERTS = 32
TOP_K = 4
EXPERT_FF = 1536
SWIGLU_LIMIT = 7.0
SWIGLU_ALPHA = 1.702
MOE_BLOCK = 128

kernel_name = "hybrid_swa_sink_hgrn2_moe_block"


def rms_norm(x, gain):
    xf = x.astype(F32)
    y = xf * lax.rsqrt(jnp.mean(xf * xf, axis=-1, keepdims=True) + EPS)
    return (y * gain.astype(F32)).astype(x.dtype)


def sliding_window_attention(q, k, v, sinks):
    B, S, Hq, hd = q.shape
    Hkv = k.shape[2]
    G = Hq // Hkv
    nb = S // WINDOW
    qb = q.reshape(B, nb, WINDOW, Hkv, G, hd).astype(F32)
    kb = k.reshape(B, nb, WINDOW, Hkv, hd)
    vb = v.reshape(B, nb, WINDOW, Hkv, hd)
    zk = jnp.zeros_like(kb[:, :1])
    k2 = jnp.concatenate([jnp.concatenate([zk, kb[:, :-1]], axis=1), kb], axis=2)
    v2 = jnp.concatenate([jnp.concatenate([zk, vb[:, :-1]], axis=1), vb], axis=2)
    scores = jnp.einsum('bnqhgd,bnshd->bnhgqs', qb, k2.astype(F32)) * (hd ** -0.5)
    qpos = jnp.arange(WINDOW) + WINDOW
    kpos = jnp.arange(2 * WINDOW)
    band = (kpos[None, :] <= qpos[:, None]) & (qpos[:, None] - kpos[None, :] < WINDOW)
    has_prev = (jnp.arange(nb) > 0)[:, None, None] | (kpos >= WINDOW)[None, None, :]
    mask = (band[None] & has_prev)[None, :, None, None]
    scores = jnp.where(mask, scores, -jnp.inf)
    sink = sinks.astype(F32).reshape(Hkv, G)[None, None, :, :, None, None]
    m = jnp.maximum(jnp.max(scores, axis=-1, keepdims=True), sink)
    p = jnp.exp(scores - m)
    p = p / (jnp.sum(p, axis=-1, keepdims=True) + jnp.exp(sink - m))
    out = jnp.einsum('bnhgqs,bnshd->bnqhgd', p.astype(v.dtype), v2)
    return out.reshape(B, S, Hq * hd)


def hgrn2_recurrence(q, k, v, log_f):
    B, S, H, dk = q.shape
    dv = v.shape[-1]
    C = HGRN_CHUNK
    nc = S // C

    def to_chunks(t):
        return t.reshape(B, nc, C, H, t.shape[-1]).transpose(1, 0, 3, 2, 4)

    causal = jnp.tril(jnp.ones((C, C), dtype=bool))[None, None, :, :, None]

    def step(state, xs):
        qc, kc, vc, gc = xs
        b = jnp.cumsum(gc, axis=2)
        o_inter = jnp.einsum('bhtd,bhde->bhte', qc * jnp.exp(b), state)
        rel = jnp.where(causal, b[:, :, :, None, :] - b[:, :, None, :, :], -jnp.inf)
        attn = jnp.einsum('bhtd,bhsd,bhtsd->bhts', qc, kc, jnp.exp(rel))
        o_intra = jnp.einsum('bhts,bhse->bhte', attn, vc)
        b_last = b[:, :, -1:, :]
        new_state = (jnp.exp(b_last[:, :, 0, :])[..., None] * state
                     + jnp.einsum('bhsd,bhse->bhde', kc * jnp.exp(b_last - b), vc))
        return new_state, o_inter + o_intra

    s0 = jnp.zeros((B, H, dk, dv), F32)
    _, o = lax.scan(step, s0, (to_chunks(q), to_chunks(k), to_chunks(v), to_chunks(log_f)))
    return o.transpose(1, 0, 3, 2, 4).reshape(B, S, H, dv)


def moe_ffn(x, w_router, b_router, w_gate_up, b_gate_up, w_down, b_down):
    B, S, D = x.shape
    T = B * S
    xt = x.reshape(T, D)
    logits = (xt @ w_router + b_router).astype(F32)
    top_val, top_idx = lax.top_k(logits, TOP_K)
    gates = jax.nn.softmax(top_val, axis=-1)
    flat_e = top_idx.reshape(-1)
    flat_tok = jnp.arange(T * TOP_K) // TOP_K
    order = jnp.argsort(flat_e)
    sorted_e = flat_e[order]
    sorted_tok = flat_tok[order]
    sorted_w = gates.reshape(-1)[order]
    counts = jnp.bincount(flat_e, length=N_EXPERTS)
    padded = (counts + MOE_BLOCK - 1) // MOE_BLOCK * MOE_BLOCK
    pad_end = jnp.cumsum(padded)
    pad_start = pad_end - padded
    start = jnp.cumsum(counts) - counts
    dest = pad_start[sorted_e] + (jnp.arange(T * TOP_K) - start[sorted_e])
    n_rows = T * TOP_K + N_EXPERTS * MOE_BLOCK
    n_blocks = n_rows // MOE_BLOCK
    row_tok = jnp.zeros((n_rows,), jnp.int32).at[dest].set(sorted_tok.astype(jnp.int32))
    row_w = jnp.zeros((n_rows,), F32).at[dest].set(sorted_w)
    block_e = jnp.minimum(jnp.searchsorted(pad_end, jnp.arange(n_blocks) * MOE_BLOCK, side='right'),
                          N_EXPERTS - 1)
    rows = xt[row_tok].reshape(n_blocks, MOE_BLOCK, D)

    def expert_block(args):
        xb, e = args
        hgu = xb @ w_gate_up[e] + b_gate_up[e]
        gate = jnp.minimum(hgu[:, :EXPERT_FF], SWIGLU_LIMIT)
        up = jnp.clip(hgu[:, EXPERT_FF:], -SWIGLU_LIMIT, SWIGLU_LIMIT)
        act = (up + 1.0) * gate * jax.nn.sigmoid(SWIGLU_ALPHA * gate)
        return act @ w_down[e] + b_down[e]

    out = lax.map(expert_block, (rows, block_e)).reshape(n_rows, D)
    y = jax.ops.segment_sum(out * row_w[:, None].astype(out.dtype), row_tok, num_segments=T)
    return y.reshape(B, S, D)


def setup_inputs(seed: int = 0) -> dict:
    key = jax.random.key(seed)
    ks = jax.random.split(key, 18)
    L = DEPTH

    def nrm(k, shape, scale):
        return jax.random.normal(k, shape, F32) * scale

    return {
        "x": nrm(ks[0], (BATCH, SEQ, D_MODEL), 1.0),
        "norm_mix": 1.0 + nrm(ks[1], (L, D_MODEL), 0.02),
        "w_in": nrm(ks[2], (L, D_MODEL, IN_WIDTH), D_MODEL ** -0.5),
        "q_norm": 1.0 + nrm(ks[3], (L, ATT_HEAD_DIM), 0.02),
        "k_norm": 1.0 + nrm(ks[4], (L, ATT_HEAD_DIM), 0.02),
        "attn_sinks": nrm(ks[5], (L, ATT_HEADS), 1.0),
        "lb_logits": nrm(ks[6], (L + 1, HGRN_K_W), 0.5),
        "hgrn_norm": 1.0 + nrm(ks[7], (L, HGRN_VAL_DIM), 0.02),
        "w_proj_attn": nrm(ks[8], (L, ATT_Q_W, D_MODEL), ATT_Q_W ** -0.5),
        "w_proj_hgrn": nrm(ks[9], (L, HGRN_V_W, D_MODEL), HGRN_V_W ** -0.5),
        "w_out": nrm(ks[10], (L, D_MODEL, D_MODEL), D_MODEL ** -0.5),
        "norm_ffn": 1.0 + nrm(ks[11], (L, D_MODEL), 0.02),
        "w_router": nrm(ks[12], (L, D_MODEL, N_EXPERTS), D_MODEL ** -0.5),
        "b_router": nrm(ks[13], (L, N_EXPERTS), 0.01),
        "w_gate_up": nrm(ks[14], (L, N_EXPERTS, D_MODEL, 2 * EXPERT_FF), D_MODEL ** -0.5),
        "b_gate_up": nrm(ks[15], (L, N_EXPERTS, 2 * EXPERT_FF), 0.01),
        "w_down": nrm(ks[16], (L, N_EXPERTS, EXPERT_FF, D_MODEL), EXPERT_FF ** -0.5),
        "b_down": nrm(ks[17], (L, N_EXPERTS, D_MODEL), 0.01),
    }


def reference(x, norm_mix, w_in, q_norm, k_norm, attn_sinks, lb_logits, hgrn_norm,
              w_proj_attn, w_proj_hgrn, w_out, norm_ffn, w_router, b_router,
              w_gate_up, b_gate_up, w_down, b_down):
    B, S, D = x.shape
    lb_all = jnp.cumsum(jax.nn.softmax(lb_logits.astype(F32), axis=0), axis=0)
    h = x
    for l in range(DEPTH):
        xn = rms_norm(h, norm_mix[l])
        proj = xn @ w_in[l]
        q_a, k_a, v_a, q_h, f_h, i_h, g_h, gate_a, gate_h = jnp.split(proj, IN_SPLITS, axis=-1)

        q_a = rms_norm(q_a.reshape(B, S, ATT_HEADS, ATT_HEAD_DIM), q_norm[l])
        k_a = rms_norm(k_a.reshape(B, S, ATT_KV_HEADS, ATT_HEAD_DIM), k_norm[l])
        v_a = v_a.reshape(B, S, ATT_KV_HEADS, ATT_HEAD_DIM)
        y_a = sliding_window_attention(q_a, k_a, v_a, attn_sinks[l])

        lb = lb_all[l]
        z = f_h.astype(F32)
        log_f = jnp.log(lb + (1.0 - lb) * jax.nn.sigmoid(z))
        k_h = (1.0 - lb) * jax.nn.sigmoid(-z)
        q_hf = jax.nn.silu(q_h.astype(F32)) * (HGRN_KEY_DIM ** -0.5)
        hs = (B, S, HGRN_HEADS, HGRN_KEY_DIM)
        vs = (B, S, HGRN_HEADS, HGRN_VAL_DIM)
        o_h = hgrn2_recurrence(q_hf.reshape(hs), k_h.reshape(hs),
                               i_h.astype(F32).reshape(vs), log_f.reshape(hs))
        o_h = rms_norm(o_h, hgrn_norm[l]) * jax.nn.silu(g_h.astype(F32)).reshape(vs)
        y_h = o_h.reshape(B, S, HGRN_V_W).astype(x.dtype)

        merged = (jax.nn.sigmoid(gate_a) * (y_a @ w_proj_attn[l])
                  + jax.nn.sigmoid(gate_h) * (y_h @ w_proj_hgrn[l]))
        h = h + merged @ w_out[l]

        hn = rms_norm(h, norm_ffn[l])
        h = h + moe_ffn(hn, w_router[l], b_router[l], w_gate_up[l], b_gate_up[l],
                        w_down[l], b_down[l])
    return h
```

```python
import functools

import jax
import jax.numpy as jnp
from jax import lax
from jax.experimental import pallas as pl
from jax.experimental.pallas import tpu as pltpu

F32 = jnp.float32
BF16 = jnp.bfloat16
I32 = jnp.int32
U32 = jnp.uint32

EPS = 1e-6
ATT_HEAD_DIM = 64
ATT_GROUP = 4
WINDOW = 128
HGRN_DIM = 128
HGRN_CHUNK = 64
HGRN_SUB = 16
EXP_CLAMP = 80.0
N_EXPERTS = 32
TOP_K = 4
SWIGLU_LIMIT = 7.0
SWIGLU_ALPHA = 1.702
NEG_BIG = -1e30

VMEM_LIMIT_BYTES = 56 * 1024 * 1024


def _params(sem):
    return pltpu.CompilerParams(dimension_semantics=sem, vmem_limit_bytes=VMEM_LIMIT_BYTES)


def _sigmoid(x):
    return 1.0 / (1.0 + jnp.exp(-x))


def _split3(x):
    hi = x.astype(BF16)
    r1 = x - hi.astype(F32)
    mid = r1.astype(BF16)
    lo = (r1 - mid.astype(F32)).astype(BF16)
    return hi, mid, lo


def _rmsnorm_kernel(x_ref, g_ref, o_ref):
    x = x_ref[...]
    ms = jnp.mean(x * x, axis=-1, keepdims=True)
    o_ref[...] = (x * lax.rsqrt(ms + EPS) * g_ref[...]).astype(o_ref.dtype)


def _rmsnorm(x2d, gain, tm):
    t, d = x2d.shape
    return pl.pallas_call(
        _rmsnorm_kernel,
        out_shape=jax.ShapeDtypeStruct((t, d), BF16),
        grid=(t // tm,),
        in_specs=[pl.BlockSpec((tm, d), lambda i: (i, 0)),
                  pl.BlockSpec((1, d), lambda i: (0, 0))],
        out_specs=pl.BlockSpec((tm, d), lambda i: (i, 0)),
        compiler_params=_params(("parallel",)),
        name="rmsnorm",
    )(x2d, gain.reshape(1, d))


def _mm_kernel(*refs, has_res):
    if has_res:
        x_ref, w_ref, r_ref, o_ref, wb_ref = refs
    else:
        x_ref, w_ref, o_ref, wb_ref = refs

    @pl.when(pl.program_id(1) == 0)
    def _():
        wb_ref[...] = w_ref[...].astype(BF16)

    acc = jnp.dot(x_ref[...], wb_ref[...], preferred_element_type=F32)
    if has_res:
        acc = acc + r_ref[...]
    o_ref[...] = acc.astype(o_ref.dtype)


def _matmul(x_bf16, w_f32, res, tm, tn, name):
    m, k = x_bf16.shape
    n = w_f32.shape[1]
    in_specs = [pl.BlockSpec((tm, k), lambda j, i: (i, 0)),
                pl.BlockSpec((k, tn), lambda j, i: (0, j))]
    args = [x_bf16, w_f32]
    if res is not None:
        in_specs.append(pl.BlockSpec((tm, tn), lambda j, i: (i, j)))
        args.append(res)
    return pl.pallas_call(
        functools.partial(_mm_kernel, has_res=res is not None),
        out_shape=jax.ShapeDtypeStruct((m, n), F32),
        grid=(n // tn, m // tm),
        in_specs=in_specs,
        out_specs=pl.BlockSpec((tm, tn), lambda j, i: (i, j)),
        scratch_shapes=[pltpu.VMEM((k, tn), BF16)],
        compiler_params=_params(("arbitrary", "arbitrary")),
        name=name,
    )(*args)


def _group_mean_matrix(width, group):
    r = lax.broadcasted_iota(I32, (width, width), 0) // group
    c = lax.broadcasted_iota(I32, (width, width), 1) // group
    return jnp.where(r == c, 1.0 / group, 0.0).astype(BF16)


def _head_rmsnorm(x, gain, mean_mat):
    sq = x * x
    hi = sq.astype(BF16)
    lo = (sq - hi.astype(F32)).astype(BF16)
    ms = (jnp.dot(hi, mean_mat, preferred_element_type=F32)
          + jnp.dot(lo, mean_mat, preferred_element_type=F32))
    return x * lax.rsqrt(ms + EPS) * gain


def _attn_kernel(sink_ref, q_ref, kc_ref, kp_ref, vc_ref, vp_ref, qg_ref, kg_ref, o_ref, *, n_kv):
    hd, grp, w = ATT_HEAD_DIM, ATT_GROUP, WINDOW
    qw = grp * hd
    n = pl.program_id(1)
    mean_mat = _group_mean_matrix(qw, hd)

    rows = lax.broadcasted_iota(I32, (grp * w, 2 * w), 0)
    cols = lax.broadcasted_iota(I32, (grp * w, 2 * w), 1)
    qi = rows & (w - 1)
    valid = (cols > qi) & (cols <= qi + w) & ((cols >= w) | (n > 0))
    row1 = lax.broadcasted_iota(I32, (grp * w, 1), 0)
    ones = jnp.ones((2 * w, hd), BF16)

    kw = n_kv * hd
    if kw % qw == 0:
        k_chunks = [slice(c * qw, (c + 1) * qw) for c in range(kw // qw)]
        k_mat = mean_mat
    else:
        k_chunks = [slice(0, kw)]
        k_mat = _group_mean_matrix(kw, hd)
    kg = kg_ref[...]
    kc = jnp.concatenate([_head_rmsnorm(kc_ref[:, s], kg[:, s], k_mat) for s in k_chunks], axis=1)
    kp = jnp.concatenate([_head_rmsnorm(kp_ref[:, s], kg[:, s], k_mat) for s in k_chunks], axis=1)
    vc = vc_ref[...]
    vp = vp_ref[...]

    for j in range(n_kv):
        qj = _head_rmsnorm(q_ref[:, j * qw:(j + 1) * qw], qg_ref[...], mean_mat) * (hd ** -0.5)
        qs = jnp.concatenate([qj[:, g * hd:(g + 1) * hd] for g in range(grp)], axis=0).astype(BF16)
        hs = slice(j * hd, (j + 1) * hd)
        k2 = jnp.concatenate([kp[:, hs], kc[:, hs]], axis=0).astype(BF16)
        v2 = jnp.concatenate([vp[:, hs], vc[:, hs]], axis=0).astype(BF16)
        v2 = jnp.concatenate([v2, ones], axis=1)
        s = lax.dot_general(qs, k2, (((1,), (1,)), ((), ())), preferred_element_type=F32)
        s = jnp.where(valid, s, NEG_BIG)
        sink = jnp.zeros((grp * w, 1), F32)
        for g in range(grp):
            sink = jnp.where(row1 // w == g, sink_ref[j * grp + g], sink)
        m = jnp.maximum(jnp.max(s, axis=1, keepdims=True), sink)
        p = jnp.exp(s - m).astype(BF16)
        pv = jnp.dot(p, v2, preferred_element_type=F32)
        out = pv[:, :hd] / (pv[:, hd:] + jnp.exp(sink - m))
        o_ref[:, j * qw:(j + 1) * qw] = jnp.concatenate(
            [out[g * w:(g + 1) * w] for g in range(grp)], axis=1).astype(o_ref.dtype)


def _attention(proj, q_gain, k_gain, sinks, batch, seq, n_heads):
    t = proj.shape[0]
    hd, grp, w = ATT_HEAD_DIM, ATT_GROUP, WINDOW
    n_kv = n_heads // grp
    qw_all, kw = n_heads * hd, n_kv * hd
    nb = seq // w
    kblk = qw_all // kw
    cur = lambda b, n, s: b * nb + n
    prev = lambda b, n, s: b * nb + jnp.maximum(n - 1, 0)
    grid_spec = pltpu.PrefetchScalarGridSpec(
        num_scalar_prefetch=1,
        grid=(batch, nb),
        in_specs=[
            pl.BlockSpec((w, qw_all), lambda b, n, s: (cur(b, n, s), 0)),
            pl.BlockSpec((w, kw), lambda b, n, s: (cur(b, n, s), kblk)),
            pl.BlockSpec((w, kw), lambda b, n, s: (prev(b, n, s), kblk)),
            pl.BlockSpec((w, kw), lambda b, n, s: (cur(b, n, s), kblk + 1)),
            pl.BlockSpec((w, kw), lambda b, n, s: (prev(b, n, s), kblk + 1)),
            pl.BlockSpec((1, grp * hd), lambda b, n, s: (0, 0)),
            pl.BlockSpec((1, kw), lambda b, n, s: (0, 0)),
        ],
        out_specs=pl.BlockSpec((w, qw_all), lambda b, n, s: (cur(b, n, s), 0)),
    )
    return pl.pallas_call(
        functools.partial(_attn_kernel, n_kv=n_kv),
        out_shape=jax.ShapeDtypeStruct((t, qw_all), BF16),
        grid_spec=grid_spec,
        compiler_params=_params(("parallel", "arbitrary")),
        name="swa_attention",
    )(sinks.astype(F32), proj, proj, proj, proj, proj,
      jnp.tile(q_gain, grp).reshape(1, grp * hd), jnp.tile(k_gain, n_kv).reshape(1, kw))


def _hgrn_kernel(q_ref, f_ref, i_ref, g_ref, lbl_ref, gn_ref, o_ref, st_ref, *, n_chunks):
    c, sub, dk = HGRN_CHUNK, HGRN_SUB, HGRN_DIM

    @pl.when(pl.program_id(2) == 0)
    def _():
        st_ref[...] = jnp.zeros_like(st_ref)

    lbl = lbl_ref[...]
    e = jnp.exp(lbl - jnp.max(lbl, axis=0, keepdims=True))
    lb = e[0:1] / jnp.sum(e, axis=0, keepdims=True)
    one_m_lb = 1.0 - lb

    row = lax.broadcasted_iota(I32, (c, c), 0)
    col = lax.broadcasted_iota(I32, (c, c), 1)
    causal = col <= row
    tri = jnp.where(causal, 1.0, 0.0)
    btri = jnp.where(causal & (col >= (row // sub) * sub), 1.0, 0.0)
    cum_mat = jnp.concatenate([tri, btri], axis=0).astype(BF16)
    gn = gn_ref[...]

    st = st_ref[...]
    for ci in range(n_chunks):
        sl = slice(ci * c, (ci + 1) * c)
        z = f_ref[sl, :]
        log_f = jnp.log(lb + one_m_lb * _sigmoid(z))
        k = one_m_lb * _sigmoid(-z)
        qh = q_ref[sl, :]
        q = qh * _sigmoid(qh) * (dk ** -0.5)
        v = i_ref[sl, :].astype(BF16)

        cs = jnp.dot(cum_mat, jnp.concatenate(_split3(log_f), axis=1), preferred_element_type=F32)
        cs = cs[:, :dk] + cs[:, dk:2 * dk] + cs[:, 2 * dk:]
        b, bq = cs[:c], cs[c:]

        qt = (q * jnp.exp(bq)).astype(BF16)
        a_rows = []
        for i in range(c // sub):
            r_i = jnp.zeros((1, dk), F32) if i == 0 else b[i * sub - 1:i * sub]
            kt = (k * jnp.exp(jnp.minimum(r_i - b, EXP_CLAMP))).astype(BF16)
            a_rows.append(lax.dot_general(qt[i * sub:(i + 1) * sub], kt, (((1,), (1,)), ((), ())),
                                          preferred_element_type=F32))
        a = jnp.where(causal, jnp.concatenate(a_rows, axis=0), 0.0).astype(BF16)
        o = jnp.dot(a, v, preferred_element_type=F32)

        qb = (q * jnp.exp(b)).astype(BF16)
        o = o + lax.dot_general(qb, st.astype(BF16), (((1,), (1,)), ((), ())), preferred_element_type=F32)
        b_last = b[c - 1:c]
        kh = (k * jnp.exp(b_last - b)).astype(BF16)
        st = st * jnp.exp(b_last) + lax.dot_general(v, kh, (((0,), (0,)), ((), ())),
                                                    preferred_element_type=F32)

        gh = g_ref[sl, :]
        ms = jnp.mean(o * o, axis=-1, keepdims=True)
        o_ref[sl, :] = (o * lax.rsqrt(ms + EPS) * gn * (gh * _sigmoid(gh))).astype(o_ref.dtype)
    st_ref[...] = st


def _hgrn(proj, lb_logits, gain, batch, seq, n_heads, col0, lc):
    t = proj.shape[0]
    d = HGRN_DIM
    steps = seq // lc
    blk0 = col0 // d

    def spec(part):
        return pl.BlockSpec((lc, d), lambda b, h, s: (b * steps + s, blk0 + part * n_heads + h))

    n_lb = lb_logits.shape[0]
    return pl.pallas_call(
        functools.partial(_hgrn_kernel, n_chunks=lc // HGRN_CHUNK),
        out_shape=jax.ShapeDtypeStruct((t, n_heads * d), BF16),
        grid=(batch, n_heads, steps),
        in_specs=[spec(0), spec(1), spec(2), spec(3),
                  pl.BlockSpec((n_lb, d), lambda b, h, s: (0, h)),
                  pl.BlockSpec((1, d), lambda b, h, s: (0, 0))],
        out_specs=pl.BlockSpec((lc, d), lambda b, h, s: (b * steps + s, h)),
        scratch_shapes=[pltpu.VMEM((d, d), F32)],
        compiler_params=_params(("parallel", "parallel", "arbitrary")),
        name="hgrn2",
    )(proj, proj, proj, proj, lb_logits.astype(F32), gain.reshape(1, d))


def _merge_kernel(ya_ref, yh_ref, wa_ref, wh_ref, ga_ref, gh_ref, o_ref, wab_ref, whb_ref):
    @pl.when(pl.program_id(1) == 0)
    def _():
        wab_ref[...] = wa_ref[...].astype(BF16)
        whb_ref[...] = wh_ref[...].astype(BF16)

    a = jnp.dot(ya_ref[...], wab_ref[...], preferred_element_type=F32)
    h = jnp.dot(yh_ref[...], whb_ref[...], preferred_element_type=F32)
    o_ref[...] = (_sigmoid(ga_ref[...]) * a + _sigmoid(gh_ref[...]) * h).astype(o_ref.dtype)


def _merge(y_a, y_h, w_a, w_h, proj, gate_col0, tm, tn):
    t, ka = y_a.shape
    kh = y_h.shape[1]
    d = w_a.shape[1]
    gblk = gate_col0 // tn
    return pl.pallas_call(
        _merge_kernel,
        out_shape=jax.ShapeDtypeStruct((t, d), BF16),
        grid=(d // tn, t // tm),
        in_specs=[pl.BlockSpec((tm, ka), lambda j, i: (i, 0)),
                  pl.BlockSpec((tm, kh), lambda j, i: (i, 0)),
                  pl.BlockSpec((ka, tn), lambda j, i: (0, j)),
                  pl.BlockSpec((kh, tn), lambda j, i: (0, j)),
                  pl.BlockSpec((tm, tn), lambda j, i: (i, gblk + j)),
                  pl.BlockSpec((tm, tn), lambda j, i: (i, gblk + d // tn + j))],
        out_specs=pl.BlockSpec((tm, tn), lambda j, i: (i, j)),
        scratch_shapes=[pltpu.VMEM((ka, tn), BF16), pltpu.VMEM((kh, tn), BF16)],
        compiler_params=_params(("arbitrary", "arbitrary")),
        name="gated_merge",
    )(y_a, y_h, w_a, w_h, proj, proj)


def _router_kernel(h_ref, g_ref, wr_ref, br_ref, hp_ref, idx_ref, gate_ref):
    h = h_ref[...]
    d = h.shape[1]
    ms = jnp.mean(h * h, axis=-1, keepdims=True)
    hn = h * lax.rsqrt(ms + EPS) * g_ref[...]

    hi_bits = pltpu.bitcast(hn[:, :d // 2].astype(BF16).astype(F32), U32)
    lo_bits = pltpu.bitcast(hn[:, d // 2:].astype(BF16).astype(F32), U32)
    hp_ref[...] = hi_bits | (lo_bits >> 16)

    x_hi, x_mid, _ = _split3(hn)
    w_hi, w_mid, _ = _split3(wr_ref[...])
    logits = (jnp.dot(x_hi, w_hi, preferred_element_type=F32)
              + jnp.dot(x_mid, w_hi, preferred_element_type=F32)
              + jnp.dot(x_hi, w_mid, preferred_element_type=F32)) + br_ref[...]

    n_e = logits.shape[1]
    lane = lax.broadcasted_iota(I32, logits.shape, 1)
    lane_k = lax.broadcasted_iota(I32, idx_ref.shape, 1)
    vals = logits
    idx_out = jnp.zeros(idx_ref.shape, I32)
    exp_out = jnp.zeros(gate_ref.shape, F32)
    denom = jnp.zeros((logits.shape[0], 1), F32)
    top = None
    for kk in range(TOP_K):
        m = jnp.max(vals, axis=1, keepdims=True)
        am = jnp.min(jnp.where(vals == m, lane, n_e), axis=1, keepdims=True)
        top = m if kk == 0 else top
        ek = jnp.exp(m - top)
        denom = denom + ek
        idx_out = jnp.where(lane_k == kk, am, idx_out)
        exp_out = jnp.where(lane_k == kk, ek, exp_out)
        vals = jnp.where(lane == am, -jnp.inf, vals)
    idx_ref[...] = idx_out
    gate_ref[...] = exp_out / denom


def _router(h2d, gain, w_router, b_router, tm):
    t, d = h2d.shape
    n_e = w_router.shape[1]
    return pl.pallas_call(
        _router_kernel,
        out_shape=(jax.ShapeDtypeStruct((t, d // 2), U32),
                   jax.ShapeDtypeStruct((t, TOP_K), I32),
                   jax.ShapeDtypeStruct((t, TOP_K), F32)),
        grid=(t // tm,),
        in_specs=[pl.BlockSpec((tm, d), lambda i: (i, 0)),
                  pl.BlockSpec((1, d), lambda i: (0, 0)),
                  pl.BlockSpec((d, n_e), lambda i: (0, 0)),
                  pl.BlockSpec((1, n_e), lambda i: (0, 0))],
        out_specs=(pl.BlockSpec((tm, d // 2), lambda i: (i, 0)),
                   pl.BlockSpec((tm, TOP_K), lambda i: (i, 0)),
                   pl.BlockSpec((tm, TOP_K), lambda i: (i, 0))),
        compiler_params=_params(("parallel",)),
        name="ffn_norm_router",
    )(h2d, gain.reshape(1, d), w_router, b_router.reshape(1, n_e))


def _index_tile_copy(idx_hbm, idx_smem, sem, tile, slot):
    return pltpu.make_async_copy(idx_hbm.at[tile], idx_smem.at[slot], sem.at[slot])


def _gather_kernel(idx_hbm, src_hbm, o_ref, idx_smem, idx_sem, row_sem, *, rows):
    i = pl.program_id(0)
    n = pl.num_programs(0)
    slot = i % 2

    @pl.when(i == 0)
    def _():
        _index_tile_copy(idx_hbm, idx_smem, idx_sem, 0, 0).start()

    _index_tile_copy(idx_hbm, idx_smem, idx_sem, i, slot).wait()

    @pl.when(i + 1 < n)
    def _():
        _index_tile_copy(idx_hbm, idx_smem, idx_sem, i + 1, 1 - slot).start()

    def issue(r, carry):
        tok = idx_smem[slot, r]
        pltpu.make_async_copy(src_hbm.at[pl.ds(tok, 1)], o_ref.at[pl.ds(r, 1)], row_sem).start()
        return carry

    lax.fori_loop(0, rows, issue, 0)
    pltpu.make_async_copy(src_hbm.at[pl.ds(0, rows)], o_ref, row_sem).wait()


def _gather_rows(src, row_idx, rows):
    r_total = row_idx.shape[0]
    wd = src.shape[1]
    return pl.pallas_call(
        functools.partial(_gather_kernel, rows=rows),
        out_shape=jax.ShapeDtypeStruct((r_total, wd), src.dtype),
        grid=(r_total // rows,),
        in_specs=[pl.BlockSpec(memory_space=pl.ANY), pl.BlockSpec(memory_space=pl.ANY)],
        out_specs=pl.BlockSpec((rows, wd), lambda i: (i, 0)),
        scratch_shapes=[pltpu.SMEM((2, rows), I32), pltpu.SemaphoreType.DMA((2,)), pltpu.SemaphoreType.DMA],
        compiler_params=_params(("arbitrary",)),
        name="moe_row_gather",
    )(row_idx.reshape(r_total // rows, rows), src)


def _unpack_rows(xp):
    hi = pltpu.bitcast(xp & jnp.uint32(0xFFFF0000), F32).astype(BF16)
    lo = pltpu.bitcast(xp << 16, F32).astype(BF16)
    return hi, lo


def _gate_up_kernel(e_ref, wt_ref, ot_ref, m_ref, first_ref, valid_ref,
                    x_ref, wg_ref, wu_ref, bg_ref, bu_ref, o_ref, wgb_ref, wub_ref):
    s = pl.program_id(0)

    @pl.when(first_ref[s] == 1)
    def _():
        wgb_ref[...] = wg_ref[0].astype(BF16)
        wub_ref[...] = wu_ref[0].astype(BF16)

    @pl.when(valid_ref[s] == 1)
    def _():
        hi, lo = _unpack_rows(x_ref[...])
        half = hi.shape[1]
        gate = (jnp.dot(hi, wgb_ref[:half], preferred_element_type=F32)
                + jnp.dot(lo, wgb_ref[half:], preferred_element_type=F32) + bg_ref[0])
        up = (jnp.dot(hi, wub_ref[:half], preferred_element_type=F32)
              + jnp.dot(lo, wub_ref[half:], preferred_element_type=F32) + bu_ref[0])
        gate = jnp.minimum(gate, SWIGLU_LIMIT)
        up = jnp.clip(up, -SWIGLU_LIMIT, SWIGLU_LIMIT)
        o_ref[...] = ((up + 1.0) * gate * _sigmoid(SWIGLU_ALPHA * gate)).astype(o_ref.dtype)

    @pl.when(valid_ref[s] == 0)
    def _():
        o_ref[...] = jnp.zeros_like(o_ref)


def _down_kernel(e_ref, wt_ref, ot_ref, m_ref, first_ref, valid_ref, a_ref, wd_ref, bd_ref, o_ref, wdb_ref):
    s = pl.program_id(0)

    @pl.when(first_ref[s] == 1)
    def _():
        wdb_ref[...] = wd_ref[0].astype(BF16)

    @pl.when(valid_ref[s] == 1)
    def _():
        o_ref[...] = jnp.dot(a_ref[...], wdb_ref[...], preferred_element_type=F32) + bd_ref[0]

    @pl.when(valid_ref[s] == 0)
    def _():
        o_ref[...] = jnp.zeros_like(o_ref)


def _work_list(blocks_per_expert, n_blocks, n_tiles):
    n_e = blocks_per_expert.shape[0]
    cb_end = jnp.cumsum(blocks_per_expert)
    cb_start = cb_end - blocks_per_expert
    total = cb_end[-1]
    s = jnp.arange(n_blocks * n_tiles, dtype=I32)
    e = jnp.minimum(jnp.searchsorted(cb_end * n_tiles, s, side="right"), n_e - 1).astype(I32)
    local = s - cb_start[e] * n_tiles
    nb_e = jnp.maximum(blocks_per_expert[e], 1)
    valid = s < total * n_tiles
    tile = jnp.where(valid, local // nb_e, 0)
    blk = jnp.where(valid, cb_start[e] + local % nb_e, 0)
    first = valid & (local % nb_e == 0)
    last = jnp.maximum(total * n_tiles - 1, 0)
    pad = s - total * n_tiles
    e = jnp.where(valid, e, e[last])
    w_tile = jnp.where(valid, tile, tile[last])
    o_tile = jnp.where(valid, tile, pad % n_tiles)
    blk = jnp.where(valid, blk, jnp.minimum(total + pad // n_tiles, n_blocks - 1))
    return (e.astype(I32), w_tile.astype(I32), o_tile.astype(I32), blk.astype(I32),
            first.astype(I32), valid.astype(I32))


def _gate_up(rows_packed, w_gate_up, b_gate_up, work, tm, tf):
    n_rows, half = rows_packed.shape
    n_e, d, two_f = w_gate_up.shape
    ff = two_f // 2
    n_f = ff // tf
    e, w_tile, o_tile, blk, first, valid = work
    b3 = b_gate_up.reshape(n_e, 1, two_f)
    grid_spec = pltpu.PrefetchScalarGridSpec(
        num_scalar_prefetch=6,
        grid=(e.shape[0],),
        in_specs=[
            pl.BlockSpec((tm, half), lambda s, e, wt, ot, m, fi, va: (m[s], 0)),
            pl.BlockSpec((1, d, tf), lambda s, e, wt, ot, m, fi, va: (e[s], 0, wt[s])),
            pl.BlockSpec((1, d, tf), lambda s, e, wt, ot, m, fi, va: (e[s], 0, n_f + wt[s])),
            pl.BlockSpec((1, 1, tf), lambda s, e, wt, ot, m, fi, va: (e[s], 0, wt[s])),
            pl.BlockSpec((1, 1, tf), lambda s, e, wt, ot, m, fi, va: (e[s], 0, n_f + wt[s])),
        ],
        out_specs=pl.BlockSpec((tm, tf), lambda s, e, wt, ot, m, fi, va: (m[s], ot[s])),
        scratch_shapes=[pltpu.VMEM((d, tf), BF16), pltpu.VMEM((d, tf), BF16)],
    )

    return pl.pallas_call(
        _gate_up_kernel,
        out_shape=jax.ShapeDtypeStruct((n_rows, ff), BF16),
        grid_spec=grid_spec,
        compiler_params=_params(("arbitrary",)),
        name="moe_gate_up",
    )(e, w_tile, o_tile, blk, first, valid, rows_packed, w_gate_up, w_gate_up, b3, b3)


def _down(act, w_down, b_down, work, tm, tn):
    n_rows, ff = act.shape
    n_e, _, d = w_down.shape
    e, w_tile, o_tile, blk, first, valid = work
    b3 = b_down.reshape(n_e, 1, d)
    grid_spec = pltpu.PrefetchScalarGridSpec(
        num_scalar_prefetch=6,
        grid=(e.shape[0],),
        in_specs=[
            pl.BlockSpec((tm, ff), lambda s, e, wt, ot, m, fi, va: (m[s], 0)),
            pl.BlockSpec((1, ff, tn), lambda s, e, wt, ot, m, fi, va: (e[s], 0, wt[s])),
            pl.BlockSpec((1, 1, tn), lambda s, e, wt, ot, m, fi, va: (e[s], 0, wt[s])),
        ],
        out_specs=pl.BlockSpec((tm, tn), lambda s, e, wt, ot, m, fi, va: (m[s], ot[s])),
        scratch_shapes=[pltpu.VMEM((ff, tn), BF16)],
    )

    return pl.pallas_call(
        _down_kernel,
        out_shape=jax.ShapeDtypeStruct((n_rows, d), F32),
        grid_spec=grid_spec,
        compiler_params=_params(("arbitrary",)),
        name="moe_down",
    )(e, w_tile, o_tile, blk, first, valid, act, w_down, b3)


def _combine_kernel(idx_hbm, rows_hbm, h_ref, gate_ref, o_ref, buf_ref, idx_smem, idx_sem, row_sem, *, tc):
    i = pl.program_id(0)
    n = pl.num_programs(0)
    slot = i % 2

    @pl.when(i == 0)
    def _():
        _index_tile_copy(idx_hbm, idx_smem, idx_sem, 0, 0).start()

    _index_tile_copy(idx_hbm, idx_smem, idx_sem, i, slot).wait()

    @pl.when(i + 1 < n)
    def _():
        _index_tile_copy(idx_hbm, idx_smem, idx_sem, i + 1, 1 - slot).start()

    def issue(r, carry):
        row = idx_smem[slot, r]
        pltpu.make_async_copy(rows_hbm.at[pl.ds(row, 1)], buf_ref.at[pl.ds(r, 1)], row_sem).start()
        return carry

    lax.fori_loop(0, TOP_K * tc, issue, 0)
    pltpu.make_async_copy(rows_hbm.at[pl.ds(0, TOP_K * tc)], buf_ref, row_sem).wait()

    gates = gate_ref[...]
    acc = h_ref[...]
    for kk in range(TOP_K):
        acc = acc + gates[:, kk:kk + 1] * buf_ref[kk * tc:(kk + 1) * tc, :]
    o_ref[...] = acc


def _combine(h2d, gates, dest, expert_rows, tc):
    t, d = h2d.shape
    idx = dest.reshape(t // tc, tc, TOP_K).transpose(0, 2, 1).reshape(t // tc, TOP_K * tc)
    return pl.pallas_call(
        functools.partial(_combine_kernel, tc=tc),
        out_shape=jax.ShapeDtypeStruct((t, d), F32),
        grid=(t // tc,),
        in_specs=[pl.BlockSpec(memory_space=pl.ANY), pl.BlockSpec(memory_space=pl.ANY),
                  pl.BlockSpec((tc, d), lambda i: (i, 0)),
                  pl.BlockSpec((tc, TOP_K), lambda i: (i, 0))],
        out_specs=pl.BlockSpec((tc, d), lambda i: (i, 0)),
        scratch_shapes=[pltpu.VMEM((TOP_K * tc, d), F32), pltpu.SMEM((2, TOP_K * tc), I32),
                        pltpu.SemaphoreType.DMA((2,)), pltpu.SemaphoreType.DMA],
        compiler_params=_params(("arbitrary",)),
        name="moe_combine",
    )(idx, expert_rows, h2d, gates)


def _routing(top_idx, tm):
    t = top_idx.shape[0]
    flat_e = top_idx.reshape(-1)
    onehot = (flat_e[:, None] == jnp.arange(N_EXPERTS, dtype=I32)[None, :]).astype(I32)
    csum = jnp.cumsum(onehot, axis=0)
    counts = csum[-1]
    rank = jnp.take_along_axis(csum, flat_e[:, None], axis=1)[:, 0] - 1
    blocks = (counts + tm - 1) // tm
    pad_start = (jnp.cumsum(blocks) - blocks) * tm
    dest = (pad_start[flat_e] + rank).astype(I32)
    n_rows = t * TOP_K + N_EXPERTS * tm
    row_tok = jnp.zeros((n_rows,), I32).at[dest].set(jnp.arange(t * TOP_K, dtype=I32) // TOP_K)
    return dest.reshape(t, TOP_K), row_tok, blocks.astype(I32), n_rows // tm


def _tile(n, pref):
    tile = min(pref, n)
    while n % tile:
        tile //= 2
    return tile


def kernel(x, norm_mix, w_in, q_norm, k_norm, attn_sinks, lb_logits, hgrn_norm, w_proj_attn, w_proj_hgrn,
           w_out, norm_ffn, w_router, b_router, w_gate_up, b_gate_up, w_down, b_down):
    batch, seq, d = x.shape
    assert norm_mix.shape[0] == 1, "one decoder layer"
    t = batch * seq
    att_w = w_proj_attn.shape[1]
    n_att_heads = att_w // ATT_HEAD_DIM
    kv_w = (n_att_heads // ATT_GROUP) * ATT_HEAD_DIM
    hg_w = w_proj_hgrn.shape[1]
    n_hg_heads = hg_w // HGRN_DIM
    hg_col0 = att_w + 2 * kv_w
    gate_col0 = hg_col0 + 4 * hg_w
    ff = w_down.shape[2]

    x2d = x.reshape(t, d)
    xn = _rmsnorm(x2d, norm_mix[0], _tile(t, 256))
    proj = _matmul(xn, w_in[0], None, _tile(t, 1024), _tile(w_in.shape[2], 512), "in_proj")

    y_a = _attention(proj, q_norm[0], k_norm[0], attn_sinks[0], batch, seq, n_att_heads)
    y_h = _hgrn(proj, lb_logits, hgrn_norm[0], batch, seq, n_hg_heads, hg_col0, _tile(seq, 512))

    tn = _tile(d, 512)
    while gate_col0 % tn:
        tn //= 2
    merged = _merge(y_a, y_h, w_proj_attn[0], w_proj_hgrn[0], proj, gate_col0, _tile(t, 512), tn)
    h = _matmul(merged, w_out[0], x2d, _tile(t, 1024), _tile(d, 512), "out_proj")

    hp, top_idx, gates = _router(h, norm_ffn[0], w_router[0], b_router[0], _tile(t, 256))
    tm = _tile(t, 512)
    dest, row_tok, blocks, n_blocks = _routing(top_idx, tm)
    rows = _gather_rows(hp, row_tok, tm)
    tf = _tile(ff, 256)
    act = _gate_up(rows, w_gate_up[0], b_gate_up[0], _work_list(blocks, n_blocks, ff // tf), tm, tf)
    tn_d = _tile(d, 1024)
    out_rows = _down(act, w_down[0], b_down[0], _work_list(blocks, n_blocks, d // tn_d), tm, tn_d)
    y = _combine(h, gates, dest, out_rows, _tile(t, 64))
    return y.reshape(batch, seq, d)
```
